```python
import jax, jax.numpy as jnp
from jax import lax
import numpy as np

D_MODEL = 1024
BATCH = 8
SEQ = 4096
DEPTH = 1

D_MIX = 2 * D_MODEL
D_SSD = D_MIX // 2
D_ATT = D_MIX - D_SSD
SSD_HEADDIM = 64
SSD_HEADS = D_SSD // SSD_HEADDIM
SSD_GROUPS = 2
SSD_STATE = 128
CONV_K = 4
CHUNK = 128
ATT_HEAD_DIM = 64
ATT_Q_HEADS = D_ATT // ATT_HEAD_DIM
ATT_KV_HEADS = 4
WINDOW = 128
ROPE_THETA = 500000.0
ROPE_DIM = ATT_HEAD_DIM // 4
ALPHA = (2.0 * DEPTH) ** 0.25
BETA = (8.0 * DEPTH) ** -0.25
LN_EPS = 1e-5
RMS_EPS = 1e-5

D_BC = SSD_GROUPS * SSD_STATE
D_XBC = D_SSD + 2 * D_BC
D_KV = ATT_KV_HEADS * ATT_HEAD_DIM
OFF_Z = 0
OFF_XBC = OFF_Z + D_SSD
OFF_DT = OFF_XBC + D_XBC
OFF_Q = OFF_DT + SSD_HEADS
OFF_K = OFF_Q + D_ATT
OFF_V = OFF_K + D_KV
OFF_G = OFF_V + D_KV
D_IN_PROJ = OFF_G + D_ATT
SPLIT_IDX = (OFF_XBC, OFF_DT, OFF_Q, OFF_K, OFF_V, OFF_G)

kernel_name = "hybrid_ssd_swa_sink_deepnorm"


def layer_norm(x, g, b):
    xf = x.astype(jnp.float32)
    mu = jnp.mean(xf, axis=-1, keepdims=True)
    var = jnp.mean(jnp.square(xf - mu), axis=-1, keepdims=True)
    y = (xf - mu) * lax.rsqrt(var + LN_EPS) * g.astype(jnp.float32) + b.astype(jnp.float32)
    return y.astype(x.dtype)


def rope_tables(positions):
    inv = ROPE_THETA ** (-jnp.arange(0, ROPE_DIM, 2, dtype=jnp.float32) / ROPE_DIM)
    ang = positions.astype(jnp.float32)[..., None] * inv
    return jnp.cos(ang), jnp.sin(ang)


def apply_partial_rope(t, cos, sin):
    rot = t[..., :ROPE_DIM].astype(jnp.float32)
    rest = t[..., ROPE_DIM:]
    r1, r2 = rot[..., :ROPE_DIM // 2], rot[..., ROPE_DIM // 2:]
    c, s = cos[:, :, None, :], sin[:, :, None, :]
    rot = jnp.concatenate([r1 * c - r2 * s, r2 * c + r1 * s], axis=-1)
    return jnp.concatenate([rot.astype(t.dtype), rest], axis=-1)


def causal_depthwise_conv(u, w, b):
    c = u.shape[-1]
    out = lax.conv_general_dilated(
        u, w[:, None, :].astype(u.dtype), window_strides=(1,),
        padding=[(CONV_K - 1, 0)], dimension_numbers=("NWC", "WIO", "NWC"),
        feature_group_count=c)
    return out + b.astype(u.dtype)


def gated_rmsnorm(y, z, w):
    yf = (y.astype(jnp.float32) * jax.nn.silu(z.astype(jnp.float32)))
    yg = yf.reshape(*yf.shape[:-1], SSD_GROUPS, D_SSD // SSD_GROUPS)
    yg = yg * lax.rsqrt(jnp.mean(jnp.square(yg), axis=-1, keepdims=True) + RMS_EPS)
    return yg.reshape(yf.shape) * w.astype(jnp.float32)


def ssd_chunked(X, dA, Bc, Cc):
    a = jnp.moveaxis(dA, 2, -1).astype(jnp.float32)
    a_cs = jnp.cumsum(a, axis=-1)
    T = a.shape[-1]
    causal = jnp.tril(jnp.ones((T, T), dtype=bool))
    seg = a_cs[..., :, None] - a_cs[..., None, :]
    Lmat = jnp.where(causal, jnp.exp(jnp.where(causal, seg, 0.0)), 0.0)
    cb = jnp.einsum("bclgn,bcsgn->bcgls", Cc, Bc).astype(jnp.float32)
    M = cb[:, :, :, None] * Lmat
    y_diag = jnp.einsum("bcgrls,bcsgrp->bclgrp", M, X)
    decay_states = jnp.moveaxis(jnp.exp(a_cs[..., -1:] - a_cs), -1, 2)
    states = jnp.einsum("bclgn,bclgrp->bcgrpn", Bc, X * decay_states[..., None]).astype(jnp.float32)
    chunk_decay = jnp.exp(a_cs[..., -1])

    def step(h, inp):
        s_c, d_c = inp
        return h * d_c[..., None, None] + s_c, h

    h0 = jnp.zeros(states.shape[:1] + states.shape[2:], jnp.float32)
    _, prev = lax.scan(step, h0, (jnp.moveaxis(states, 1, 0), jnp.moveaxis(chunk_decay, 1, 0)))
    prev = jnp.moveaxis(prev, 0, 1)
    decay_out = jnp.moveaxis(jnp.exp(a_cs), -1, 2)
    y_off = jnp.einsum("bclgn,bcgrpn->bclgrp", Cc, prev) * decay_out[..., None]
    return y_diag + y_off


def ssd_branch(z, xbc, dt, conv_w, conv_b, dt_bias, a_log, d_skip, norm_w):
    b, L, _ = xbc.shape
    R = SSD_HEADS // SSD_GROUPS
    nc = L // CHUNK
    xbc = jax.nn.silu(causal_depthwise_conv(xbc, conv_w, conv_b))
    xs, Bm, Cm = jnp.split(xbc, [D_SSD, D_SSD + D_BC], axis=-1)
    dt = jax.nn.softplus(dt.astype(jnp.float32) + dt_bias.astype(jnp.float32))
    A = -jnp.exp(a_log.astype(jnp.float32))
    xh = xs.reshape(b, L, SSD_GROUPS, R, SSD_HEADDIM)
    dth = dt.reshape(b, L, SSD_GROUPS, R)
    X = (xh.astype(jnp.float32) * dth[..., None]).reshape(b, nc, CHUNK, SSD_GROUPS, R, SSD_HEADDIM)
    dA = (dth * A.reshape(SSD_GROUPS, R)).reshape(b, nc, CHUNK, SSD_GROUPS, R)
    Bc = Bm.reshape(b, nc, CHUNK, SSD_GROUPS, SSD_STATE)
    Cc = Cm.reshape(b, nc, CHUNK, SSD_GROUPS, SSD_STATE)
    y = ssd_chunked(X, dA, Bc, Cc).reshape(b, L, SSD_GROUPS, R, SSD_HEADDIM)
    y = y + d_skip.astype(jnp.float32).reshape(SSD_GROUPS, R, 1) * xh.astype(jnp.float32)
    y = gated_rmsnorm(y.reshape(b, L, D_SSD), z, norm_w)
    return y.astype(z.dtype)


def swa_branch(q, k, v, g, cos, sin, sinks):
    b, L, _ = q.shape
    nb = L // WINDOW
    R = ATT_Q_HEADS // ATT_KV_HEADS
    q = apply_partial_rope(q.reshape(b, L, ATT_Q_HEADS, ATT_HEAD_DIM), cos, sin)
    k = apply_partial_rope(k.reshape(b, L, ATT_KV_HEADS, ATT_HEAD_DIM), cos, sin)
    v = v.reshape(b, L, ATT_KV_HEADS, ATT_HEAD_DIM)
    qb = q.reshape(b, nb, WINDOW, ATT_KV_HEADS, R, ATT_HEAD_DIM)
    kb = k.reshape(b, nb, WINDOW, ATT_KV_HEADS, ATT_HEAD_DIM)
    vb = v.reshape(b, nb, WINDOW, ATT_KV_HEADS, ATT_HEAD_DIM)
    pad = ((0, 0), (1, 0), (0, 0), (0, 0), (0, 0))
    kk = jnp.concatenate([jnp.pad(kb[:, :-1], pad), kb], axis=2)
    vv = jnp.concatenate([jnp.pad(vb[:, :-1], pad), vb], axis=2)
    scale = ATT_HEAD_DIM ** -0.5
    s = jnp.einsum("bnqkrd,bnskd->bnkrqs", qb, kk).astype(jnp.float32) * scale
    qi = jnp.arange(WINDOW)[:, None]
    si = jnp.arange(2 * WINDOW)[None, :]
    band = (si > qi) & (si <= qi + WINDOW)
    valid = band[None] & ((jnp.arange(nb)[:, None, None] > 0) | (si[None] >= WINDOW))
    s = jnp.where(valid[None, :, None, None], s, -jnp.inf)
    sink = sinks.astype(jnp.float32).reshape(ATT_KV_HEADS, R)[None, None, :, :, None, None]
    m = jnp.maximum(jnp.max(s, axis=-1, keepdims=True), sink)
    p = jnp.exp(s - m)
    denom = jnp.sum(p, axis=-1, keepdims=True) + jnp.exp(sink - m)
    o = jnp.einsum("bnkrqs,bnskd->bnqkrd", (p / denom).astype(vv.dtype), vv)
    o = o.reshape(b, L, D_ATT)
    return (o * jax.nn.silu(g.astype(jnp.float32))).astype(q.dtype)


def hybrid_mixer(x, cos, sin, w_in, conv_w, conv_b, dt_bias, a_log, d_skip, ssd_norm_w, attn_sinks, w_out):
    proj = jnp.einsum("bld,de->ble", x, w_in)
    z, xbc, dt, q, k, v, g = jnp.split(proj, SPLIT_IDX, axis=-1)
    y_ssd = ssd_branch(z, xbc, dt, conv_w, conv_b, dt_bias, a_log, d_skip, ssd_norm_w)
    y_att = swa_branch(q, k, v, g, cos, sin, attn_sinks)
    y = jnp.concatenate([y_ssd, y_att], axis=-1)
    return jnp.einsum("ble,ed->bld", y, w_out)


def setup_inputs(seed: int = 0) -> dict:
    key = jax.random.key(seed)
    ks = jax.random.split(key, 12)
    x = jax.random.normal(ks[0], (BATCH, SEQ, D_MODEL), jnp.float32)
    positions = jnp.broadcast_to(jnp.arange(SEQ, dtype=jnp.int32)[None, :], (BATCH, SEQ))
    col_scale = jnp.ones((D_IN_PROJ,), jnp.float32).at[OFF_V:OFF_V + D_KV].set(BETA)
    w_in = jax.random.normal(ks[1], (DEPTH, D_MODEL, D_IN_PROJ), jnp.float32) * (D_MODEL ** -0.5) * col_scale
    conv_w = jax.random.normal(ks[2], (DEPTH, CONV_K, D_XBC), jnp.float32) * (CONV_K ** -0.5)
    conv_b = 0.01 * jax.random.normal(ks[3], (DEPTH, D_XBC), jnp.float32)
    dt0 = jnp.exp(jax.random.uniform(ks[4], (DEPTH, SSD_HEADS), jnp.float32,
                                     minval=float(np.log(1e-3)), maxval=float(np.log(1e-1))))
    dt_bias = dt0 + jnp.log(-jnp.expm1(-dt0))
    a_log = jnp.log(jax.random.uniform(ks[5], (DEPTH, SSD_HEADS), jnp.float32, minval=1.0, maxval=16.0))
    d_skip = 1.0 + 0.1 * jax.random.normal(ks[6], (DEPTH, SSD_HEADS), jnp.float32)
    ssd_norm_w = 1.0 + 0.02 * jax.random.normal(ks[7], (DEPTH, D_SSD), jnp.float32)
    attn_sinks = 0.5 * jax.random.normal(ks[8], (DEPTH, ATT_Q_HEADS), jnp.float32)
    w_out = jax.random.normal(ks[9], (DEPTH, D_MIX, D_MODEL), jnp.float32) * (D_MIX ** -0.5) * BETA
    ln_g = 1.0 + 0.02 * jax.random.normal(ks[10], (DEPTH, D_MODEL), jnp.float32)
    ln_b = 0.02 * jax.random.normal(ks[11], (DEPTH, D_MODEL), jnp.float32)
    return {"x": x, "positions": positions, "w_in": w_in, "conv_w": conv_w, "conv_b": conv_b,
            "dt_bias": dt_bias, "a_log": a_log, "d_skip": d_skip, "ssd_norm_w": ssd_norm_w,
            "attn_sinks": attn_sinks, "w_out": w_out, "ln_g": ln_g, "ln_b": ln_b}


def reference(x, positions, w_in, conv_w, conv_b, dt_bias, a_log, d_skip, ssd_norm_w,
              attn_sinks, w_out, ln_g, ln_b):
    cos, sin = rope_tables(positions)
    for l in range(DEPTH):
        h = hybrid_mixer(x, cos, sin, w_in[l], conv_w[l], conv_b[l], dt_bias[l], a_log[l],
                         d_skip[l], ssd_norm_w[l], attn_sinks[l], w_out[l])
        x = layer_norm(ALPHA * x + h.astype(x.dtype), ln_g[l], ln_b[l])
    return x
```

```python
import functools

import numpy as np
import jax
import jax.numpy as jnp
from jax import lax
from jax.experimental import pallas as pl
from jax.experimental.pallas import tpu as pltpu

D_MODEL = 1024
DEPTH = 1
D_MIX = 2 * D_MODEL
D_SSD = D_MIX // 2
D_ATT = D_MIX - D_SSD
SSD_HEADDIM = 64
SSD_HEADS = D_SSD // SSD_HEADDIM
SSD_GROUPS = 2
SSD_STATE = 128
CONV_K = 4
CHUNK = 128
ATT_HEAD_DIM = 64
ATT_Q_HEADS = D_ATT // ATT_HEAD_DIM
ATT_KV_HEADS = 4
WINDOW = 128
ROPE_THETA = 500000.0
ROPE_DIM = ATT_HEAD_DIM // 4
ALPHA = (2.0 * DEPTH) ** 0.25
LN_EPS = 1e-5
RMS_EPS = 1e-5
D_BC = SSD_GROUPS * SSD_STATE
D_XBC = D_SSD + 2 * D_BC
D_KV = ATT_KV_HEADS * ATT_HEAD_DIM
OFF_Z = 0
OFF_XBC = OFF_Z + D_SSD
OFF_DT = OFF_XBC + D_XBC
OFF_Q = OFF_DT + SSD_HEADS
OFF_K = OFF_Q + D_ATT
OFF_V = OFF_K + D_KV
OFF_G = OFF_V + D_KV
D_IN_PROJ = OFF_G + D_ATT

LANES = 128
SUBLANES = 8
VMEM_LIMIT_BYTES = 56 * 1024 * 1024

HEADS_PER_GROUP = SSD_HEADS // SSD_GROUPS
GROUP_WIDTH = HEADS_PER_GROUP * SSD_HEADDIM
Q_PER_KV = ATT_Q_HEADS // ATT_KV_HEADS
MASK_VALUE = -1e30

SEG_Z = (0, D_SSD)
SEG_XBC = (SEG_Z[1], SEG_Z[1] + D_XBC)
SEG_Q = (SEG_XBC[1], SEG_XBC[1] + D_ATT)
SEG_K = (SEG_Q[1], SEG_Q[1] + D_KV)
SEG_V = (SEG_K[1], SEG_K[1] + D_KV)
SEG_G = (SEG_V[1], SEG_V[1] + D_ATT)
SEG_DT = (SEG_G[1], SEG_G[1] + LANES)
W_CAT_COLS = SEG_DT[1]


def _silu(v):
    return v * (1.0 / (1.0 + jnp.exp(-v)))


def _softplus(v):
    return jnp.maximum(v, 0.0) + jnp.log1p(jnp.exp(-jnp.abs(v)))


def _rope_slab(t, cos_t, sin_lo, sin_hi):
    up = pltpu.roll(t, LANES - ROPE_DIM // 2, axis=1)
    down = pltpu.roll(t, ROPE_DIM // 2, axis=1)
    return t * cos_t + up * sin_lo + down * sin_hi


def _in_proj_kernel(x_ref, pos_ref, inv_ref, w_ref, z_ref, xbc_ref, q_ref, k_ref, v_ref, g_ref, dt_ref):
    xb = x_ref[...].astype(jnp.bfloat16)

    def seg(bounds):
        return jnp.dot(xb, w_ref[:, bounds[0]:bounds[1]], preferred_element_type=jnp.float32)

    z_ref[...] = seg(SEG_Z).astype(z_ref.dtype)
    xbc_ref[...] = seg(SEG_XBC).astype(xbc_ref.dtype)
    v_ref[...] = seg(SEG_V).astype(v_ref.dtype)
    g_ref[...] = seg(SEG_G).astype(g_ref.dtype)
    dt_ref[...] = seg(SEG_DT)

    lane = lax.broadcasted_iota(jnp.int32, (1, LANES), 1)
    in_head = lane % ATT_HEAD_DIM
    ang = pos_ref[...] * inv_ref[...]
    cos_t = jnp.cos(ang)
    sin_t = jnp.sin(ang)
    sin_lo = jnp.where(in_head < ROPE_DIM // 2, -sin_t, 0.0)
    sin_hi = jnp.where((in_head >= ROPE_DIM // 2) & (in_head < ROPE_DIM), sin_t, 0.0)

    scale = ATT_HEAD_DIM ** -0.5
    q = seg(SEG_Q)
    for s in range(D_ATT // LANES):
        sl = slice(s * LANES, (s + 1) * LANES)
        q_ref[:, sl] = (_rope_slab(q[:, sl], cos_t, sin_lo, sin_hi) * scale).astype(q_ref.dtype)
    k = seg(SEG_K)
    for s in range(D_KV // LANES):
        sl = slice(s * LANES, (s + 1) * LANES)
        k_ref[:, sl] = _rope_slab(k[:, sl], cos_t, sin_lo, sin_hi).astype(k_ref.dtype)


def _in_proj(x2, pos2, inv_row, w_cat, tm):
    n = x2.shape[0]
    row = lambda i: (i, 0)
    outs = [(D_SSD, jnp.bfloat16), (D_XBC, jnp.bfloat16), (D_ATT, jnp.bfloat16),
            (D_KV, jnp.bfloat16), (D_KV, jnp.bfloat16), (D_ATT, jnp.bfloat16),
            (LANES, jnp.float32)]
    return pl.pallas_call(
        _in_proj_kernel,
        grid=(n // tm,),
        in_specs=[
            pl.BlockSpec((tm, D_MODEL), row),
            pl.BlockSpec((tm, 1), row),
            pl.BlockSpec((1, LANES), lambda i: (0, 0)),
            pl.BlockSpec((D_MODEL, W_CAT_COLS), lambda i: (0, 0)),
        ],
        out_specs=[pl.BlockSpec((tm, w), row) for w, _ in outs],
        out_shape=[jax.ShapeDtypeStruct((n, w), dt) for w, dt in outs],
        compiler_params=pltpu.CompilerParams(
            dimension_semantics=("arbitrary",), vmem_limit_bytes=VMEM_LIMIT_BYTES),
        name="in_proj",
    )(x2, pos2, inv_row, w_cat)


def _ssd_kernel(z_ref, xbc_ref, dt_ref, convw_ref, convb_ref, dtb_ref, alog_ref, dskip_ref,
                normw_ref, y_ref, ubuf_ref, state_ref):
    c = pl.program_id(1)

    @pl.when(c == 0)
    def _():
        ubuf_ref[0:SUBLANES, :] = jnp.zeros((SUBLANES, D_XBC), jnp.float32)
        state_ref[...] = jnp.zeros_like(state_ref)

    ubuf_ref[SUBLANES:SUBLANES + CHUNK, :] = xbc_ref[...].astype(jnp.float32)
    conv = convb_ref[...] + convw_ref[CONV_K - 1:CONV_K, :] * ubuf_ref[SUBLANES:SUBLANES + CHUNK, :]
    for kk in range(CONV_K - 1):
        off = SUBLANES - (CONV_K - 1) + kk
        conv = conv + convw_ref[kk:kk + 1, :] * ubuf_ref[off:off + CHUNK, :]
    ubuf_ref[0:SUBLANES, :] = ubuf_ref[CHUNK:CHUNK + SUBLANES, :]
    act = _silu(conv)
    xs = act[:, :D_SSD]
    xs_b = xs.astype(jnp.bfloat16)

    dt = _softplus(dt_ref[...] + dtb_ref[...])
    d_a = dt * (-jnp.exp(alog_ref[...]))
    row_i = lax.broadcasted_iota(jnp.int32, (CHUNK, CHUNK), 0)
    col_i = lax.broadcasted_iota(jnp.int32, (CHUNK, CHUNK), 1)
    causal = row_i >= col_i
    tri = jnp.where(causal, 1.0, 0.0).astype(jnp.bfloat16)
    d1 = d_a.astype(jnp.bfloat16)
    r1 = d_a - d1.astype(jnp.float32)
    d2 = r1.astype(jnp.bfloat16)
    d3 = (r1 - d2.astype(jnp.float32)).astype(jnp.bfloat16)
    cs = (jnp.dot(tri, d1, preferred_element_type=jnp.float32)
          + jnp.dot(tri, d2, preferred_element_type=jnp.float32)
          + jnp.dot(tri, d3, preferred_element_type=jnp.float32))
    cs_t = cs.T
    dt_t = dt.T
    cs_last = cs[CHUNK - 1:CHUNK, :]
    w_state = dt * jnp.exp(cs_last - cs)
    lane_lo = lax.broadcasted_iota(jnp.int32, (1, LANES), 1) < SSD_HEADDIM

    y_parts = []
    for g in range(SSD_GROUPS):
        b_g = act[:, D_SSD + g * SSD_STATE:D_SSD + (g + 1) * SSD_STATE]
        c_g = act[:, D_SSD + D_BC + g * SSD_STATE:D_SSD + D_BC + (g + 1) * SSD_STATE]
        b_gb = b_g.astype(jnp.bfloat16)
        c_gb = c_g.astype(jnp.bfloat16)
        cb = lax.dot_general(c_gb, b_gb, (((1,), (1,)), ((), ())),
                             preferred_element_type=jnp.float32)
        prev = state_ref[g]
        prev_b = prev.astype(jnp.bfloat16)
        xd_parts = []
        decay_parts = []
        for pr in range(HEADS_PER_GROUP // 2):
            pair = g * (HEADS_PER_GROUP // 2) + pr
            lhs_m = []
            lhs_c = []
            w_cols = []
            dec_cols = []
            for hh in range(2):
                h = 2 * pair + hh
                col = jnp.broadcast_to(cs[:, h:h + 1], (CHUNK, CHUNK))
                seg = col - cs_t[h:h + 1, :]
                lmat = jnp.exp(jnp.where(causal, seg, MASK_VALUE))
                lhs_m.append((cb * lmat * dt_t[h:h + 1, :]).astype(jnp.bfloat16))
                lhs_c.append((c_g * jnp.exp(col)).astype(jnp.bfloat16))
                w_cols.append(jnp.broadcast_to(w_state[:, h:h + 1], (CHUNK, LANES)))
                dec_cols.append(jnp.broadcast_to(cs_last[:, h:h + 1], (1, LANES)))
            xs_pair = xs_b[:, pair * LANES:(pair + 1) * LANES]
            prev_pair = prev_b[:, pr * LANES:(pr + 1) * LANES]
            zero = jnp.zeros_like(xs_pair)
            rhs_x = jnp.concatenate([jnp.where(lane_lo, xs_pair, zero),
                                     jnp.where(lane_lo, zero, xs_pair)], axis=0)
            rhs_p = jnp.concatenate([jnp.where(lane_lo, prev_pair, zero),
                                     jnp.where(lane_lo, zero, prev_pair)], axis=0)
            y_pair = (jnp.dot(jnp.concatenate(lhs_m, axis=1), rhs_x,
                              preferred_element_type=jnp.float32)
                      + jnp.dot(jnp.concatenate(lhs_c, axis=1), rhs_p,
                                preferred_element_type=jnp.float32))
            y_parts.append(y_pair)
            w_pair = jnp.where(lane_lo, w_cols[0], w_cols[1])
            xd_parts.append((xs[:, pair * LANES:(pair + 1) * LANES] * w_pair).astype(jnp.bfloat16))
            decay_parts.append(jnp.exp(jnp.where(lane_lo, dec_cols[0], dec_cols[1])))
        xd = jnp.concatenate(xd_parts, axis=1)
        chunk_decay = jnp.concatenate(decay_parts, axis=1)
        new_states = jnp.dot(b_g.T.astype(jnp.bfloat16), xd,
                             preferred_element_type=jnp.float32)
        state_ref[g] = prev * chunk_decay + new_states

    y = jnp.concatenate(y_parts, axis=1) + dskip_ref[...] * xs
    yf = y * _silu(z_ref[...].astype(jnp.float32))
    outs = []
    for g in range(SSD_GROUPS):
        yg = yf[:, g * GROUP_WIDTH:(g + 1) * GROUP_WIDTH]
        ms = jnp.mean(yg * yg, axis=-1, keepdims=True)
        outs.append(yg * lax.rsqrt(ms + RMS_EPS))
    y_ref[...] = (jnp.concatenate(outs, axis=1) * normw_ref[...]).astype(y_ref.dtype)


def _ssd(z, xbc, dt, conv_w, conv_b, dtb_pad, alog_pad, dskip_e, norm_w, batch, nc):
    n = z.shape[0]
    tok = lambda b, c: (b * nc + c, 0)
    const = lambda b, c: (0, 0)
    return pl.pallas_call(
        _ssd_kernel,
        grid=(batch, nc),
        in_specs=[
            pl.BlockSpec((CHUNK, D_SSD), tok),
            pl.BlockSpec((CHUNK, D_XBC), tok),
            pl.BlockSpec((CHUNK, LANES), tok),
            pl.BlockSpec((CONV_K, D_XBC), const),
            pl.BlockSpec((1, D_XBC), const),
            pl.BlockSpec((1, LANES), const),
            pl.BlockSpec((1, LANES), const),
            pl.BlockSpec((1, D_SSD), const),
            pl.BlockSpec((1, D_SSD), const),
        ],
        out_specs=pl.BlockSpec((CHUNK, D_SSD), tok),
        out_shape=jax.ShapeDtypeStruct((n, D_SSD), jnp.bfloat16),
        scratch_shapes=[
            pltpu.VMEM((CHUNK + SUBLANES, D_XBC), jnp.float32),
            pltpu.VMEM((SSD_GROUPS, SSD_STATE, GROUP_WIDTH), jnp.float32),
        ],
        compiler_params=pltpu.CompilerParams(
            dimension_semantics=("arbitrary", "arbitrary"), vmem_limit_bytes=VMEM_LIMIT_BYTES),
        name="ssd",
    )(z, xbc, dt, conv_w, conv_b, dtb_pad, alog_pad, dskip_e, norm_w)


def _swa_kernel(sink_ref, q_ref, kc_ref, kp_ref, vc_ref, vp_ref, g_ref, o_ref):
    j = pl.program_id(1)
    rows = Q_PER_KV * WINDOW
    qi = lax.broadcasted_iota(jnp.int32, (rows, 2 * WINDOW), 0) % WINDOW
    si = lax.broadcasted_iota(jnp.int32, (rows, 2 * WINDOW), 1)
    valid = (si > qi) & (si <= qi + WINDOW) & ((j > 0) | (si >= WINDOW))
    head_of_row = lax.broadcasted_iota(jnp.int32, (rows, 1), 0) // WINDOW

    for kh in range(ATT_KV_HEADS):
        ksl = slice(kh * ATT_HEAD_DIM, (kh + 1) * ATT_HEAD_DIM)
        kk = jnp.concatenate([kp_ref[:, ksl], kc_ref[:, ksl]], axis=0)
        vv = jnp.concatenate([vp_ref[:, ksl], vc_ref[:, ksl]], axis=0)
        q4 = jnp.concatenate(
            [q_ref[:, (kh * Q_PER_KV + r) * ATT_HEAD_DIM:(kh * Q_PER_KV + r + 1) * ATT_HEAD_DIM]
             for r in range(Q_PER_KV)], axis=0)
        s = lax.dot_general(q4, kk, (((1,), (1,)), ((), ())),
                            preferred_element_type=jnp.float32)
        s = jnp.where(valid, s, -jnp.inf)
        sink = jnp.zeros((rows, 1), jnp.float32)
        for r in range(Q_PER_KV):
            sink = jnp.where(head_of_row == r, sink_ref[kh * Q_PER_KV + r], sink)
        m = jnp.maximum(jnp.max(s, axis=-1, keepdims=True), sink)
        p = jnp.exp(s - m)
        denom = jnp.sum(p, axis=-1, keepdims=True) + jnp.exp(sink - m)
        o = jnp.dot(p.astype(jnp.bfloat16), vv, preferred_element_type=jnp.float32)
        o = o * (1.0 / denom)
        for r in range(Q_PER_KV):
            hsl = slice((kh * Q_PER_KV + r) * ATT_HEAD_DIM, (kh * Q_PER_KV + r + 1) * ATT_HEAD_DIM)
            gate = _silu(g_ref[:, hsl].astype(jnp.float32))
            o_ref[:, hsl] = (o[r * WINDOW:(r + 1) * WINDOW, :] * gate).astype(o_ref.dtype)


def _swa(sinks, q, k, v, g, batch, nb):
    n = q.shape[0]
    cur = lambda b, j: (b * nb + j, 0)
    prev = lambda b, j: (b * nb + jnp.maximum(j - 1, 0), 0)
    return pl.pallas_call(
        _swa_kernel,
        grid=(batch, nb),
        in_specs=[
            pl.BlockSpec(memory_space=pltpu.SMEM),
            pl.BlockSpec((WINDOW, D_ATT), cur),
            pl.BlockSpec((WINDOW, D_KV), cur),
            pl.BlockSpec((WINDOW, D_KV), prev),
            pl.BlockSpec((WINDOW, D_KV), cur),
            pl.BlockSpec((WINDOW, D_KV), prev),
            pl.BlockSpec((WINDOW, D_ATT), cur),
        ],
        out_specs=pl.BlockSpec((WINDOW, D_ATT), cur),
        out_shape=jax.ShapeDtypeStruct((n, D_ATT), jnp.bfloat16),
        compiler_params=pltpu.CompilerParams(
            dimension_semantics=("arbitrary", "arbitrary"), vmem_limit_bytes=VMEM_LIMIT_BYTES),
        name="swa",
    )(sinks, q, k, k, v, v, g)


def _out_proj_kernel(ys_ref, ya_ref, x_ref, w_ref, lng_ref, lnb_ref, o_ref):
    h = (jnp.dot(ys_ref[...], w_ref[:D_SSD, :], preferred_element_type=jnp.float32)
         + jnp.dot(ya_ref[...], w_ref[D_SSD:, :], preferred_element_type=jnp.float32))
    r = ALPHA * x_ref[...] + h
    mu = jnp.mean(r, axis=-1, keepdims=True)
    d = r - mu
    var = jnp.mean(d * d, axis=-1, keepdims=True)
    o_ref[...] = d * lax.rsqrt(var + LN_EPS) * lng_ref[...] + lnb_ref[...]


def _out_proj(y_ssd, y_att, x2, w_out_b, ln_g, ln_b, tm):
    n = x2.shape[0]
    row = lambda i: (i, 0)
    const = lambda i: (0, 0)
    return pl.pallas_call(
        _out_proj_kernel,
        grid=(n // tm,),
        in_specs=[
            pl.BlockSpec((tm, D_SSD), row),
            pl.BlockSpec((tm, D_ATT), row),
            pl.BlockSpec((tm, D_MODEL), row),
            pl.BlockSpec((D_MIX, D_MODEL), const),
            pl.BlockSpec((1, D_MODEL), const),
            pl.BlockSpec((1, D_MODEL), const),
        ],
        out_specs=pl.BlockSpec((tm, D_MODEL), row),
        out_shape=jax.ShapeDtypeStruct((n, D_MODEL), jnp.float32),
        compiler_params=pltpu.CompilerParams(
            dimension_semantics=("arbitrary",), vmem_limit_bytes=VMEM_LIMIT_BYTES),
        name="out_proj",
    )(y_ssd, y_att, x2, w_out_b, ln_g, ln_b)


def _pad_lanes(v):
    return jnp.pad(v.astype(jnp.float32), (0, LANES - v.shape[0])).reshape(1, LANES)


def _layer(x2, pos2, batch, seq, w_in, conv_w, conv_b, dt_bias, a_log, d_skip, ssd_norm_w,
           attn_sinks, w_out, ln_g, ln_b):
    n = batch * seq
    tm = min(512, n)
    w_dt = jnp.pad(w_in[:, OFF_DT:OFF_Q], ((0, 0), (0, LANES - SSD_HEADS)))
    w_cat = jnp.concatenate(
        [w_in[:, OFF_Z:OFF_XBC], w_in[:, OFF_XBC:OFF_DT], w_in[:, OFF_Q:OFF_K],
         w_in[:, OFF_K:OFF_V], w_in[:, OFF_V:OFF_G], w_in[:, OFF_G:], w_dt],
        axis=1).astype(jnp.bfloat16)
    inv = ROPE_THETA ** (-jnp.arange(0, ROPE_DIM, 2, dtype=jnp.float32) / ROPE_DIM)
    inv_head = jnp.concatenate([inv, inv, jnp.zeros((ATT_HEAD_DIM - ROPE_DIM,), jnp.float32)])
    inv_row = jnp.tile(inv_head, LANES // ATT_HEAD_DIM).reshape(1, LANES)
    z, xbc, q, k, v, g, dt = _in_proj(x2, pos2, inv_row, w_cat, tm)
    y_ssd = _ssd(z, xbc, dt, conv_w.astype(jnp.float32), conv_b.reshape(1, D_XBC).astype(jnp.float32),
                 _pad_lanes(dt_bias), _pad_lanes(a_log),
                 jnp.repeat(d_skip.astype(jnp.float32), SSD_HEADDIM).reshape(1, D_SSD),
                 ssd_norm_w.reshape(1, D_SSD).astype(jnp.float32), batch, seq // CHUNK)
    y_att = _swa(attn_sinks.astype(jnp.float32), q, k, v, g, batch, seq // WINDOW)
    return _out_proj(y_ssd, y_att, x2, w_out.astype(jnp.bfloat16),
                     ln_g.reshape(1, D_MODEL).astype(jnp.float32),
                     ln_b.reshape(1, D_MODEL).astype(jnp.float32), tm)


def kernel(x, positions, w_in, conv_w, conv_b, dt_bias, a_log, d_skip, ssd_norm_w, attn_sinks,
           w_out, ln_g, ln_b):
    batch, seq, _ = x.shape
    x2 = x.reshape(batch * seq, D_MODEL)
    pos2 = positions.astype(jnp.float32).reshape(batch * seq, 1)
    for l in range(DEPTH):
        x2 = _layer(x2, pos2, batch, seq, w_in[l], conv_w[l], conv_b[l], dt_bias[l], a_log[l],
                    d_skip[l], ssd_norm_w[l], attn_sinks[l], w_out[l], ln_g[l], ln_b[l])
    return x2.reshape(batch, seq, D_MODEL)
```

```python
import functools

import numpy as np
import jax
import jax.numpy as jnp
from jax import lax
from jax.experimental import pallas as pl
from jax.experimental.pallas import tpu as pltpu

D_MODEL = 1024
DEPTH = 1
D_MIX = 2 * D_MODEL
D_SSD = D_MIX // 2
D_ATT = D_MIX - D_SSD
SSD_HEADDIM = 64
SSD_HEADS = D_SSD // SSD_HEADDIM
SSD_GROUPS = 2
SSD_STATE = 128
CONV_K = 4
CHUNK = 128
ATT_HEAD_DIM = 64
ATT_Q_HEADS = D_ATT // ATT_HEAD_DIM
ATT_KV_HEADS = 4
WINDOW = 128
ROPE_THETA = 500000.0
ROPE_DIM = ATT_HEAD_DIM // 4
ALPHA = (2.0 * DEPTH) ** 0.25
LN_EPS = 1e-5
RMS_EPS = 1e-5
D_BC = SSD_GROUPS * SSD_STATE
D_XBC = D_SSD + 2 * D_BC
D_KV = ATT_KV_HEADS * ATT_HEAD_DIM
OFF_Z = 0
OFF_XBC = OFF_Z + D_SSD
OFF_DT = OFF_XBC + D_XBC
OFF_Q = OFF_DT + SSD_HEADS
OFF_K = OFF_Q + D_ATT
OFF_V = OFF_K + D_KV
OFF_G = OFF_V + D_KV
D_IN_PROJ = OFF_G + D_ATT

LANES = 128
SUBLANES = 8
VMEM_LIMIT_BYTES = 56 * 1024 * 1024

HEADS_PER_GROUP = SSD_HEADS // SSD_GROUPS
GROUP_WIDTH = HEADS_PER_GROUP * SSD_HEADDIM
EXP_W = SSD_HEADS * CHUNK
EXP_DEC = EXP_W + D_SSD
EXP_COLS = EXP_DEC + D_SSD
Q_PER_KV = ATT_Q_HEADS // ATT_KV_HEADS
MASK_VALUE = -1e30
LOG2E = float(np.log2(np.e))

SEG_Z = (0, D_SSD)
SEG_XBC = (SEG_Z[1], SEG_Z[1] + D_XBC)
SEG_Q = (SEG_XBC[1], SEG_XBC[1] + D_ATT)
SEG_K = (SEG_Q[1], SEG_Q[1] + D_KV)
SEG_V = (SEG_K[1], SEG_K[1] + D_KV)
SEG_G = (SEG_V[1], SEG_V[1] + D_ATT)
SEG_DT = (SEG_G[1], SEG_G[1] + LANES)
W_CAT_COLS = SEG_DT[1]


def _silu(v):
    return v * (1.0 / (1.0 + jnp.exp2(v * (-LOG2E))))


def _softplus(v):
    return jnp.maximum(v, 0.0) + jnp.log1p(jnp.exp(-jnp.abs(v)))


def _split3(v):
    a = v.astype(jnp.bfloat16)
    r = v - a.astype(jnp.float32)
    b = r.astype(jnp.bfloat16)
    c = (r - b.astype(jnp.float32)).astype(jnp.bfloat16)
    return a, b, c


def _in_proj_kernel(tiles_per_seq, x_ref, cos_ref, sin_ref, w_ref, convw_ref, convb_ref,
                    sz_ref, xbc_ref, q_ref, k_ref, v_ref, sg_ref, dt_ref, ubuf_ref):
    tm = x_ref.shape[0]
    xb = x_ref[...].astype(jnp.bfloat16)

    def seg(bounds):
        return jnp.dot(xb, w_ref[:, bounds[0]:bounds[1]], preferred_element_type=jnp.float32)

    @pl.when(pl.program_id(0) % tiles_per_seq == 0)
    def _():
        ubuf_ref[0:SUBLANES, :] = jnp.zeros((SUBLANES, D_XBC), jnp.float32)

    def conv_chunk(c0, c1):
        ubuf_ref[SUBLANES:SUBLANES + tm, c0:c1] = seg((SEG_XBC[0] + c0, SEG_XBC[0] + c1))
        u = ubuf_ref[:, c0:c1]
        acc = None
        for kk in range(CONV_K):
            shift = CONV_K - 1 - kk
            ush = u if shift == 0 else pltpu.roll(u, shift, axis=0)
            ush = ush[SUBLANES:, :].reshape(tm // SUBLANES, SUBLANES, c1 - c0)
            term = ush * convw_ref[kk, :, c0:c1][None]
            acc = term if acc is None else acc + term
        acc = acc + convb_ref[:, c0:c1][None]
        xbc_ref[:, c0:c1] = _silu(acc).reshape(tm, c1 - c0).astype(xbc_ref.dtype)
        ubuf_ref[0:SUBLANES, c0:c1] = ubuf_ref[tm:tm + SUBLANES, c0:c1]

    lane = lax.broadcasted_iota(jnp.int32, (1, LANES), 1)
    first_half = (lane % ATT_HEAD_DIM) < ROPE_DIM // 2
    cos_t = cos_ref[...]
    sin_t = sin_ref[...]

    def rope(t):
        up = pltpu.roll(t, LANES - ROPE_DIM // 2, axis=1)
        down = pltpu.roll(t, ROPE_DIM // 2, axis=1)
        return t * cos_t + jnp.where(first_half, up, down) * sin_t

    third = D_XBC // 3
    conv_chunk(0, third)
    sz_ref[...] = _silu(seg(SEG_Z)).astype(sz_ref.dtype)
    conv_chunk(third, 2 * third)
    sg_ref[...] = _silu(seg(SEG_G)).astype(sg_ref.dtype)
    conv_chunk(2 * third, D_XBC)
    scale = ATT_HEAD_DIM ** -0.5 * LOG2E
    q = seg(SEG_Q)
    for s in range(D_ATT // LANES):
        sl = slice(s * LANES, (s + 1) * LANES)
        q_ref[:, sl] = (rope(q[:, sl]) * scale).astype(q_ref.dtype)
    k = seg(SEG_K)
    for s in range(D_KV // LANES):
        sl = slice(s * LANES, (s + 1) * LANES)
        k_ref[:, sl] = rope(k[:, sl]).astype(k_ref.dtype)
    v_ref[...] = seg(SEG_V).astype(v_ref.dtype)
    dt_ref[...] = seg(SEG_DT)


def _in_proj(x2, cos_t, sin_t, w_cat, conv_w, conv_b, tm, tiles_per_seq):
    n = x2.shape[0]
    row = lambda i: (i, 0)
    const = lambda i: (0, 0)
    outs = [(D_SSD, jnp.bfloat16), (D_XBC, jnp.bfloat16), (D_ATT, jnp.bfloat16),
            (D_KV, jnp.bfloat16), (D_KV, jnp.bfloat16), (D_ATT, jnp.bfloat16),
            (LANES, jnp.float32)]
    return pl.pallas_call(
        functools.partial(_in_proj_kernel, tiles_per_seq),
        grid=(n // tm,),
        in_specs=[
            pl.BlockSpec((tm, D_MODEL), row),
            pl.BlockSpec((tm, LANES), row),
            pl.BlockSpec((tm, LANES), row),
            pl.BlockSpec((D_MODEL, W_CAT_COLS), const),
            pl.BlockSpec((CONV_K, SUBLANES, D_XBC), lambda i: (0, 0, 0)),
            pl.BlockSpec((SUBLANES, D_XBC), const),
        ],
        out_specs=[pl.BlockSpec((tm, w), row) for w, _ in outs],
        out_shape=[jax.ShapeDtypeStruct((n, w), dt) for w, dt in outs],
        scratch_shapes=[pltpu.VMEM((tm + SUBLANES, D_XBC), jnp.float32)],
        compiler_params=pltpu.CompilerParams(
            dimension_semantics=("arbitrary",), vmem_limit_bytes=VMEM_LIMIT_BYTES),
        name="in_proj",
    )(x2, cos_t, sin_t, w_cat, conv_w, conv_b)


def _ssd_chunk(rows, sz_ref, xbc_ref, dt_ref, dtb_ref, alog_ref, dskip_ref, normw_ref,
               expand_ref, y_ref, states):
    xs_b = xbc_ref[rows, :D_SSD]
    xs = xs_b.astype(jnp.float32)

    dt_t = _softplus(dt_ref[rows, :].T[:SSD_HEADS, :] + dtb_ref[...])
    da_t = dt_t * (-LOG2E * jnp.exp(alog_ref[...]))
    row_i = lax.broadcasted_iota(jnp.int32, (CHUNK, CHUNK), 0)
    col_i = lax.broadcasted_iota(jnp.int32, (CHUNK, CHUNK), 1)
    causal = row_i >= col_i
    tri_t = jnp.where(row_i <= col_i, 1.0, 0.0).astype(jnp.bfloat16)
    cs_t = sum(jnp.dot(part, tri_t, preferred_element_type=jnp.float32)
               for part in _split3(da_t))
    cs_last_t = cs_t[:, CHUNK - 1:CHUNK]
    w_t = dt_t * jnp.exp2(cs_last_t - cs_t)
    src_t = cs_t - jnp.log2(dt_t)
    parts = _split3(cs_t) + _split3(w_t)[:2] + _split3(jnp.exp2(cs_t))[:2]
    stacked = jnp.concatenate(
        [p.astype(jnp.float32) for p in parts]
        + [jnp.zeros((CHUNK - len(parts) * SSD_HEADS, CHUNK), jnp.float32)], axis=0)
    tok_major = stacked.T.astype(jnp.bfloat16)
    expanded = jnp.dot(tok_major, expand_ref[...], preferred_element_type=jnp.float32)
    lane_lo = lax.broadcasted_iota(jnp.int32, (1, LANES), 1) < SSD_HEADDIM

    y_parts = []
    new_states = []
    for g in range(SSD_GROUPS):
        b_gb = xbc_ref[rows, D_SSD + g * SSD_STATE:D_SSD + (g + 1) * SSD_STATE]
        c_gb = xbc_ref[rows, D_SSD + D_BC + g * SSD_STATE:D_SSD + D_BC + (g + 1) * SSD_STATE]
        cb = lax.dot_general(c_gb, b_gb, (((1,), (1,)), ((), ())),
                             preferred_element_type=jnp.float32)
        prev = states[g]
        y_carry = jnp.dot(c_gb, prev.astype(jnp.bfloat16),
                          preferred_element_type=jnp.float32)
        xd_parts = []
        for pr in range(HEADS_PER_GROUP // 2):
            pair = g * (HEADS_PER_GROUP // 2) + pr
            lhs_m = []
            for hh in range(2):
                h = 2 * pair + hh
                seg = expanded[:, h * CHUNK:(h + 1) * CHUNK] - src_t[h:h + 1, :]
                lhs_m.append((cb * jnp.exp2(jnp.where(causal, seg, MASK_VALUE)))
                             .astype(jnp.bfloat16))
            xs_pair = xs_b[:, pair * LANES:(pair + 1) * LANES]
            zero = jnp.zeros_like(xs_pair)
            rhs_x = jnp.concatenate([jnp.where(lane_lo, xs_pair, zero),
                                     jnp.where(lane_lo, zero, xs_pair)], axis=0)
            y_pair = jnp.dot(jnp.concatenate(lhs_m, axis=1), rhs_x,
                             preferred_element_type=jnp.float32)
            w_pair = expanded[:, EXP_W + pair * LANES:EXP_W + (pair + 1) * LANES]
            dec_pair = expanded[:, EXP_DEC + pair * LANES:EXP_DEC + (pair + 1) * LANES]
            y_parts.append(y_pair + y_carry[:, pr * LANES:(pr + 1) * LANES] * dec_pair)
            xd_parts.append((xs[:, pair * LANES:(pair + 1) * LANES] * w_pair)
                            .astype(jnp.bfloat16))
        xd = jnp.concatenate(xd_parts, axis=1)
        chunk_decay = expanded[CHUNK - 1:CHUNK,
                               EXP_DEC + g * GROUP_WIDTH:EXP_DEC + (g + 1) * GROUP_WIDTH]
        b_t = b_gb.astype(jnp.float32).T.astype(jnp.bfloat16)
        new_states.append(prev * chunk_decay
                          + jnp.dot(b_t, xd, preferred_element_type=jnp.float32))

    y = jnp.concatenate(y_parts, axis=1) + dskip_ref[...] * xs
    yf = y * sz_ref[rows, :].astype(jnp.float32)
    outs = []
    for g in range(SSD_GROUPS):
        yg = yf[:, g * GROUP_WIDTH:(g + 1) * GROUP_WIDTH]
        ms = jnp.mean(yg * yg, axis=-1, keepdims=True)
        outs.append(yg * lax.rsqrt(ms + RMS_EPS))
    y_ref[rows, :] = (jnp.concatenate(outs, axis=1) * normw_ref[...]).astype(y_ref.dtype)
    return new_states


def _ssd_kernel(sz_ref, xbc_ref, dt_ref, dtb_ref, alog_ref, dskip_ref, normw_ref, expand_ref,
                y_ref, state_ref):
    @pl.when(pl.program_id(1) == 0)
    def _():
        state_ref[...] = jnp.zeros_like(state_ref)

    states = [state_ref[g] for g in range(SSD_GROUPS)]
    for ci in range(sz_ref.shape[0] // CHUNK):
        rows = slice(ci * CHUNK, (ci + 1) * CHUNK)
        states = _ssd_chunk(rows, sz_ref, xbc_ref, dt_ref, dtb_ref, alog_ref, dskip_ref,
                            normw_ref, expand_ref, y_ref, states)
    for g in range(SSD_GROUPS):
        state_ref[g] = states[g]


def _expand_matrix():
    e = np.zeros((CHUNK, EXP_COLS), np.float32)
    for h in range(SSD_HEADS):
        for term in range(3):
            e[term * SSD_HEADS + h, h * CHUNK:(h + 1) * CHUNK] = 1.0
        for term in range(2):
            e[(3 + term) * SSD_HEADS + h, EXP_W + h * SSD_HEADDIM:EXP_W + (h + 1) * SSD_HEADDIM] = 1.0
            e[(5 + term) * SSD_HEADS + h,
              EXP_DEC + h * SSD_HEADDIM:EXP_DEC + (h + 1) * SSD_HEADDIM] = 1.0
    return jnp.asarray(e, dtype=jnp.bfloat16)


def _ssd(sz, xbc, dt, dtb_col, alog_col, dskip_e, norm_w, batch, seq, tile):
    n = sz.shape[0]
    nt = seq // tile
    tok = lambda b, c: (b * nt + c, 0)
    const = lambda b, c: (0, 0)
    return pl.pallas_call(
        _ssd_kernel,
        grid=(batch, nt),
        in_specs=[
            pl.BlockSpec((tile, D_SSD), tok),
            pl.BlockSpec((tile, D_XBC), tok),
            pl.BlockSpec((tile, LANES), tok),
            pl.BlockSpec((SSD_HEADS, 1), const),
            pl.BlockSpec((SSD_HEADS, 1), const),
            pl.BlockSpec((1, D_SSD), const),
            pl.BlockSpec((1, D_SSD), const),
            pl.BlockSpec((CHUNK, EXP_COLS), const),
        ],
        out_specs=pl.BlockSpec((tile, D_SSD), tok),
        out_shape=jax.ShapeDtypeStruct((n, D_SSD), jnp.bfloat16),
        scratch_shapes=[pltpu.VMEM((SSD_GROUPS, SSD_STATE, GROUP_WIDTH), jnp.float32)],
        compiler_params=pltpu.CompilerParams(
            dimension_semantics=("arbitrary", "arbitrary"), vmem_limit_bytes=VMEM_LIMIT_BYTES),
        name="ssd",
    )(sz, xbc, dt, dtb_col, alog_col, dskip_e, norm_w, _expand_matrix())


def _swa_kernel(sink_ref, q_ref, kc_ref, kp_ref, vc_ref, vp_ref, sg_ref, o_ref):
    first_tile = pl.program_id(1) == 0
    nblk = q_ref.shape[0] // WINDOW
    cols = Q_PER_KV * WINDOW
    si = lax.broadcasted_iota(jnp.int32, (2 * WINDOW, WINDOW), 0)
    qi = lax.broadcasted_iota(jnp.int32, (2 * WINDOW, WINDOW), 1)
    band = (si > qi) & (si <= qi + WINDOW)
    bias_any = jnp.where(band, 0.0, -jnp.inf)
    bias_first = jnp.where(band & (jnp.logical_not(first_tile) | (si >= WINDOW)), 0.0, -jnp.inf)
    head_of_lane = lax.broadcasted_iota(jnp.int32, (1, cols), 1) // WINDOW
    pairs_per_kv = Q_PER_KV // 2

    for blk in range(nblk):
        rows = slice(blk * WINDOW, (blk + 1) * WINDOW)
        if blk == 0:
            k_prev, v_prev = kp_ref[...], vp_ref[...]
            bias = jnp.concatenate([bias_first] * Q_PER_KV, axis=1)
        else:
            prev_rows = slice((blk - 1) * WINDOW, blk * WINDOW)
            k_prev, v_prev = kc_ref[prev_rows, :], vc_ref[prev_rows, :]
            bias = jnp.concatenate([bias_any] * Q_PER_KV, axis=1)
        k_ext = jnp.concatenate([k_prev, kc_ref[rows, :]], axis=0)
        v_ext = jnp.concatenate([v_prev, vc_ref[rows, :]], axis=0).astype(jnp.float32)
        for kh in range(ATT_KV_HEADS):
            ksl = slice(kh * ATT_HEAD_DIM, (kh + 1) * ATT_HEAD_DIM)
            q4 = jnp.concatenate(
                [q_ref[rows, (kh * Q_PER_KV + r) * ATT_HEAD_DIM:
                       (kh * Q_PER_KV + r + 1) * ATT_HEAD_DIM] for r in range(Q_PER_KV)],
                axis=0)
            s_t = lax.dot_general(k_ext[:, ksl], q4, (((1,), (1,)), ((), ())),
                                  preferred_element_type=jnp.float32) + bias
            sink = jnp.zeros((1, cols), jnp.float32)
            for r in range(Q_PER_KV):
                sink = jnp.where(head_of_lane == r, sink_ref[kh * Q_PER_KV + r] * LOG2E, sink)
            m = jnp.maximum(jnp.max(s_t, axis=0, keepdims=True), sink)
            p_t = jnp.exp2(s_t - m)
            denom = jnp.sum(p_t, axis=0, keepdims=True) + jnp.exp2(sink - m)
            v_pair_t = v_ext[:, (kh // 2) * LANES:(kh // 2 + 1) * LANES].T
            v_t = v_pair_t[(kh % 2) * ATT_HEAD_DIM:(kh % 2 + 1) * ATT_HEAD_DIM, :]
            o_t = jnp.dot(v_t.astype(jnp.bfloat16), p_t.astype(jnp.bfloat16),
                          preferred_element_type=jnp.float32)
            o_t = o_t * (1.0 / denom)
            for pr in range(pairs_per_kv):
                slab_t = jnp.concatenate(
                    [o_t[:, (2 * pr) * WINDOW:(2 * pr + 1) * WINDOW],
                     o_t[:, (2 * pr + 1) * WINDOW:(2 * pr + 2) * WINDOW]], axis=0)
                lsl = slice((kh * pairs_per_kv + pr) * LANES, (kh * pairs_per_kv + pr + 1) * LANES)
                gate = sg_ref[rows, lsl].astype(jnp.float32)
                o_ref[rows, lsl] = (slab_t.T * gate).astype(o_ref.dtype)


def _swa(sinks, q, k, v, sg, batch, seq, tile):
    n = q.shape[0]
    nt = seq // tile
    per_tile = tile // WINDOW
    cur = lambda b, j: (b * nt + j, 0)
    prev = lambda b, j: ((b * nt + j) * per_tile - jnp.minimum(j, 1), 0)
    return pl.pallas_call(
        _swa_kernel,
        grid=(batch, nt),
        in_specs=[
            pl.BlockSpec(memory_space=pltpu.SMEM),
            pl.BlockSpec((tile, D_ATT), cur),
            pl.BlockSpec((tile, D_KV), cur),
            pl.BlockSpec((WINDOW, D_KV), prev),
            pl.BlockSpec((tile, D_KV), cur),
            pl.BlockSpec((WINDOW, D_KV), prev),
            pl.BlockSpec((tile, D_ATT), cur),
        ],
        out_specs=pl.BlockSpec((tile, D_ATT), cur),
        out_shape=jax.ShapeDtypeStruct((n, D_ATT), jnp.bfloat16),
        compiler_params=pltpu.CompilerParams(
            dimension_semantics=("arbitrary", "arbitrary"), vmem_limit_bytes=VMEM_LIMIT_BYTES),
        name="swa",
    )(sinks, q, k, k, v, v, sg)


def _out_proj_kernel(ys_ref, ya_ref, x_ref, w_ref, lng_ref, lnb_ref, o_ref):
    h = (jnp.dot(ys_ref[...], w_ref[:D_SSD, :], preferred_element_type=jnp.float32)
         + jnp.dot(ya_ref[...], w_ref[D_SSD:, :], preferred_element_type=jnp.float32))
    r = ALPHA * x_ref[...] + h
    mu = jnp.mean(r, axis=-1, keepdims=True)
    d = r - mu
    var = jnp.mean(d * d, axis=-1, keepdims=True)
    o_ref[...] = d * lax.rsqrt(var + LN_EPS) * lng_ref[...] + lnb_ref[...]


def _out_proj(y_ssd, y_att, x2, w_out_b, ln_g, ln_b, tm):
    n = x2.shape[0]
    row = lambda i: (i, 0)
    const = lambda i: (0, 0)
    return pl.pallas_call(
        _out_proj_kernel,
        grid=(n // tm,),
        in_specs=[
            pl.BlockSpec((tm, D_SSD), row),
            pl.BlockSpec((tm, D_ATT), row),
            pl.BlockSpec((tm, D_MODEL), row),
            pl.BlockSpec((D_MIX, D_MODEL), const),
            pl.BlockSpec((1, D_MODEL), const),
            pl.BlockSpec((1, D_MODEL), const),
        ],
        out_specs=pl.BlockSpec((tm, D_MODEL), row),
        out_shape=jax.ShapeDtypeStruct((n, D_MODEL), jnp.float32),
        compiler_params=pltpu.CompilerParams(
            dimension_semantics=("arbitrary",), vmem_limit_bytes=VMEM_LIMIT_BYTES),
        name="out_proj",
    )(y_ssd, y_att, x2, w_out_b, ln_g, ln_b)


def _rope_lane_tables(positions):
    inv = ROPE_THETA ** (-jnp.arange(0, ROPE_DIM, 2, dtype=jnp.float32) / ROPE_DIM)
    ang = positions.astype(jnp.float32).reshape(-1, 1) * inv
    cos, sin = jnp.cos(ang), jnp.sin(ang)
    pad_one = jnp.ones((ang.shape[0], ATT_HEAD_DIM - ROPE_DIM), jnp.float32)
    cos_head = jnp.concatenate([cos, cos, pad_one], axis=1)
    sin_head = jnp.concatenate([-sin, sin, 0.0 * pad_one], axis=1)
    reps = LANES // ATT_HEAD_DIM
    return jnp.tile(cos_head, (1, reps)), jnp.tile(sin_head, (1, reps))


def _layer(x2, cos_t, sin_t, batch, seq, w_in, conv_w, conv_b, dt_bias, a_log, d_skip,
           ssd_norm_w, attn_sinks, w_out, ln_g, ln_b):
    tm = min(512, seq)
    w_dt = jnp.pad(w_in[:, OFF_DT:OFF_Q], ((0, 0), (0, LANES - SSD_HEADS)))
    w_cat = jnp.concatenate(
        [w_in[:, OFF_Z:OFF_XBC], w_in[:, OFF_XBC:OFF_DT], w_in[:, OFF_Q:OFF_K],
         w_in[:, OFF_K:OFF_V], w_in[:, OFF_V:OFF_G], w_in[:, OFF_G:], w_dt],
        axis=1).astype(jnp.bfloat16)
    sz, xbc, q, k, v, sg, dt = _in_proj(
        x2, cos_t, sin_t, w_cat,
        jnp.broadcast_to(conv_w.astype(jnp.float32)[:, None, :], (CONV_K, SUBLANES, D_XBC)),
        jnp.broadcast_to(conv_b.astype(jnp.float32)[None, :], (SUBLANES, D_XBC)), tm, seq // tm)
    y_ssd = _ssd(sz, xbc, dt, dt_bias.reshape(SSD_HEADS, 1).astype(jnp.float32),
                 a_log.reshape(SSD_HEADS, 1).astype(jnp.float32),
                 jnp.repeat(d_skip.astype(jnp.float32), SSD_HEADDIM).reshape(1, D_SSD),
                 ssd_norm_w.reshape(1, D_SSD).astype(jnp.float32), batch, seq, tm)
    y_att = _swa(attn_sinks.astype(jnp.float32), q, k, v, sg, batch, seq, tm)
    return _out_proj(y_ssd, y_att, x2, w_out.astype(jnp.bfloat16),
                     ln_g.reshape(1, D_MODEL).astype(jnp.float32),
                     ln_b.reshape(1, D_MODEL).astype(jnp.float32), tm)


def kernel(x, positions, w_in, conv_w, conv_b, dt_bias, a_log, d_skip, ssd_norm_w, attn_sinks,
           w_out, ln_g, ln_b):
    batch, seq, _ = x.shape
    x2 = x.reshape(batch * seq, D_MODEL)
    cos_t, sin_t = _rope_lane_tables(positions)
    for l in range(DEPTH):
        x2 = _layer(x2, cos_t, sin_t, batch, seq, w_in[l], conv_w[l], conv_b[l], dt_bias[l],
                    a_log[l], d_skip[l], ssd_norm_w[l], attn_sinks[l], w_out[l], ln_g[l], ln_b[l])
    return x2.reshape(batch, seq, D_MODEL)
```
